```python
import jax
import jax.numpy as jnp
from jax import lax
import numpy as np

D_MODEL = 1024
BATCH = 2
SEQ = 8192
DEPTH = 4
DEC_BATCH = 128
DEC_SEQ = 8
PAST_LEN = 2048
PAGE_SIZE = 128

N_EVEN = (DEPTH + 1) // 2
N_ODD = DEPTH // 2
EPS = 1e-6
N_HEADS = 8
HEAD_DIM = 64
N_KV_HEADS = 2
HEADS_PER_KV = N_HEADS // N_KV_HEADS
NSA_WIDTH = N_HEADS * HEAD_DIM
KV_WIDTH = N_KV_HEADS * HEAD_DIM
N_BRANCH = 3
CMP_BLOCK = 32
CMP_STRIDE = 16
SEL_BLOCK = 64
SEL_TOP = 16
SEL_LOCAL = 2
WINDOW = 512
Q_BLOCK = 128
N_ROWS = 4
POOL_WINDOWS = (2, 4, 8, 16)
POOL_WIDTH = D_MODEL - NSA_WIDTH
POOL_GROUP = POOL_WIDTH // len(POOL_WINDOWS)
POOL_HIST = max(POOL_WINDOWS) - 1
IN_EVEN = NSA_WIDTH + 6 * KV_WIDTH + N_BRANCH * N_HEADS + POOL_WIDTH
CHUNK = 128
GMLP_WIDTH = D_MODEL
GMLP_GROUPS = 4
GMLP_GROUP_DIM = GMLP_WIDTH // GMLP_GROUPS
D_FF = 2816
N_EXPERTS = 8
TOP_K = 2
D_FF_EXPERT = 3584

kernel_name = 'hybrid_nsa_pool_gmlp_moe_step'


def rmsnorm(x, g):
    xf = x.astype(jnp.float32)
    y = xf * lax.rsqrt(jnp.mean(xf * xf, axis=-1, keepdims=True) + EPS)
    return (y * g.astype(jnp.float32)).astype(x.dtype)


def layernorm(x, g, b):
    xf = x.astype(jnp.float32)
    xc = xf - jnp.mean(xf, axis=-1, keepdims=True)
    y = xc * lax.rsqrt(jnp.mean(xc * xc, axis=-1, keepdims=True) + EPS)
    return (y * g.astype(jnp.float32) + b.astype(jnp.float32)).astype(x.dtype)


def masked_softmax(s, mask):
    s = jnp.where(mask, s.astype(jnp.float32), -jnp.inf)
    m = jnp.max(s, axis=-1, keepdims=True)
    m = jnp.where(jnp.isfinite(m), m, 0.0)
    e = jnp.where(mask, jnp.exp(s - m), 0.0)
    return e / jnp.maximum(jnp.sum(e, axis=-1, keepdims=True), 1e-30)


def swiglu(h, w1, w3, w2):
    return (jax.nn.silu(h @ w1) * (h @ w3)) @ w2


def compress(rows, w, pe):
    L = rows.shape[1]
    n_cmp = (L - CMP_BLOCK) // CMP_STRIDE + 1
    idx = jnp.arange(n_cmp)[:, None] * CMP_STRIDE + jnp.arange(CMP_BLOCK)[None, :]
    blocks = rows[:, idx] + pe[None, None, :, None, :].astype(rows.dtype)
    out = jnp.einsum('bnlgd,lde->bnge', blocks, w)
    ends = jnp.arange(n_cmp) * CMP_STRIDE + CMP_BLOCK - 1
    return out, ends


def sel_blocks(rows):
    B, L, G, Dh = rows.shape
    n_sel = -(-L // SEL_BLOCK)
    rows = jnp.pad(rows, ((0, 0), (0, n_sel * SEL_BLOCK - L), (0, 0), (0, 0)))
    return rows.reshape(B, n_sel, SEL_BLOCK, G, Dh).transpose(0, 3, 1, 2, 4)


def cmp_sel_overlap(n_cmp, n_sel):
    c0 = jnp.arange(n_cmp)[:, None] * CMP_STRIDE
    s0 = jnp.arange(n_sel)[None, :] * SEL_BLOCK
    return ((c0 < s0 + SEL_BLOCK) & (c0 + CMP_BLOCK > s0)).astype(jnp.float32)


def nsa_keys(rows, w_cmp, pe_cmp, g_kc):
    kc, c_end = compress(rows[:, :, 0], w_cmp[0], pe_cmp[0])
    vc, _ = compress(rows[:, :, 1], w_cmp[1], pe_cmp[1])
    kc = rmsnorm(kc, g_kc)
    ks = sel_blocks(rows[:, :, 2])
    vs = sel_blocks(rows[:, :, 3])
    overlap = cmp_sel_overlap(kc.shape[1], ks.shape[2])
    return kc, vc, c_end, ks, vs, overlap


def nsa_query_block(q, t_pos, gates, kc, vc, c_end, ks, vs, overlap, kw, vw, s_pos):
    scale = HEAD_DIM ** -0.5
    B, Qb = q.shape[:2]
    s_c = jnp.einsum('bqghd,bngd->bqghn', q, kc) * scale
    p_c = masked_softmax(s_c, (c_end[None, :] <= t_pos[:, None])[None, :, None, None, :])
    o_c = jnp.einsum('bqghn,bngd->bqghd', p_c.astype(vc.dtype), vc)
    n_sel = ks.shape[2]
    imp = jnp.einsum('bqgn,nj->bqgj', jnp.sum(p_c, axis=3), overlap)
    blk = jnp.arange(n_sel)[None, :]
    cur = (t_pos // SEL_BLOCK)[:, None]
    valid = blk <= cur
    forced = valid & ((blk == 0) | (blk > cur - SEL_LOCAL))
    imp = jnp.where(forced[None, :, None, :], jnp.inf, jnp.where(valid[None, :, None, :], imp, -jnp.inf))
    _, idx = lax.top_k(imp, min(SEL_TOP, n_sel))
    bi = jnp.arange(B)[:, None, None, None]
    gi = jnp.arange(ks.shape[1])[None, None, :, None]
    k_g = ks[bi, gi, idx]
    v_g = vs[bi, gi, idx]
    s_s = jnp.einsum('bqghd,bqgtkd->bqghtk', q, k_g) * scale
    tok = idx[..., None] * SEL_BLOCK + jnp.arange(SEL_BLOCK)
    m_s = (tok <= t_pos[None, :, None, None, None])[:, :, :, None]
    sh = s_s.shape
    p_s = masked_softmax(s_s.reshape(sh[:4] + (-1,)), m_s.reshape(sh[:3] + (1, -1))).reshape(sh)
    o_s = jnp.einsum('bqghtk,bqgtkd->bqghd', p_s.astype(v_g.dtype), v_g)
    s_w = jnp.einsum('bqghd,bsgd->bqghs', q, kw) * scale
    dt = t_pos[:, None] - s_pos[None, :]
    m_w = (dt >= 0) & (dt < WINDOW) & (s_pos[None, :] >= 0)
    p_w = masked_softmax(s_w, m_w[None, :, None, None, :])
    o_w = jnp.einsum('bqghs,bsgd->bqghd', p_w.astype(vw.dtype), vw)
    o = gates[..., 0:1] * o_c + gates[..., 1:2] * o_s + gates[..., 2:3] * o_w
    return o.reshape(B, Qb, NSA_WIDTH)


def nsa_prompt(q, gates, rows, win, w_cmp, pe_cmp, g_kc):
    B, S = q.shape[:2]
    kc, vc, c_end, ks, vs, overlap = nsa_keys(rows, w_cmp, pe_cmp, g_kc)
    win_pad = jnp.pad(win, ((0, 0), (WINDOW, 0), (0, 0), (0, 0), (0, 0)))

    def one_block(bidx):
        q0 = bidx * Q_BLOCK
        qb = lax.dynamic_slice_in_dim(q, q0, Q_BLOCK, axis=1)
        gb = lax.dynamic_slice_in_dim(gates, q0, Q_BLOCK, axis=1)
        wb = lax.dynamic_slice_in_dim(win_pad, q0, Q_BLOCK + WINDOW, axis=1)
        t_pos = q0 + jnp.arange(Q_BLOCK)
        s_pos = q0 - WINDOW + jnp.arange(Q_BLOCK + WINDOW)
        return nsa_query_block(qb, t_pos, gb, kc, vc, c_end, ks, vs, overlap, wb[:, :, 0], wb[:, :, 1], s_pos)

    out = lax.map(one_block, jnp.arange(S // Q_BLOCK))
    return out.transpose(1, 0, 2, 3).reshape(B, S, NSA_WIDTH)


def nsa_sample(q, gates, rows_all, win_all, past_len, w_cmp, pe_cmp, g_kc):
    T = q.shape[1]
    kc, vc, c_end, ks, vs, overlap = nsa_keys(rows_all, w_cmp, pe_cmp, g_kc)
    Lw = win_all.shape[1]
    t_pos = past_len + jnp.arange(T)
    s_pos = past_len + T - Lw + jnp.arange(Lw)
    return nsa_query_block(q, t_pos, gates, kc, vc, c_end, ks, vs, overlap, win_all[:, :, 0], win_all[:, :, 1], s_pos)


def pool_mix(u, hist, start_pos, w_pool, scale):
    B, T, P = u.shape
    ext = jnp.concatenate([hist, u], axis=1).astype(jnp.float32)
    cs = jnp.concatenate([jnp.zeros((B, 1, P), jnp.float32), jnp.cumsum(ext, axis=1)], axis=1)
    pos = (start_pos + jnp.arange(T)).astype(jnp.float32)
    cur = ext[:, POOL_HIST:]
    end = cs[:, POOL_HIST + 1:]
    groups = []
    for g, w in enumerate(POOL_WINDOWS):
        c = slice(g * POOL_GROUP, (g + 1) * POOL_GROUP)
        start = cs[:, POOL_HIST + 1 - w:POOL_HIST + 1 - w + T, c]
        cnt = jnp.minimum(float(w), pos + 1.0)[None, :, None]
        groups.append((end[..., c] - start) / cnt - cur[..., c])
    d = jnp.stack(groups, axis=2).astype(u.dtype)
    y = jnp.einsum('btgc,gce->btge', d, w_pool).reshape(B, T, P)
    return y * scale


def even_project(x, g_mix, w_in, g_q, g_k):
    B, T, _ = x.shape
    p = rmsnorm(x, g_mix) @ w_in
    o1 = NSA_WIDTH
    o2 = o1 + 6 * KV_WIDTH
    o3 = o2 + N_BRANCH * N_HEADS
    q = rmsnorm(p[..., :o1].reshape(B, T, N_KV_HEADS, HEADS_PER_KV, HEAD_DIM), g_q)
    kv = p[..., o1:o2].reshape(B, T, 6, N_KV_HEADS, HEAD_DIM)
    gates = jax.nn.sigmoid(p[..., o2:o3].reshape(B, T, N_KV_HEADS, HEADS_PER_KV, N_BRANCH))
    u = p[..., o3:]
    k_sel = rmsnorm(kv[:, :, 2], g_k[1])
    k_win = rmsnorm(kv[:, :, 4], g_k[2])
    rows = jnp.stack([kv[:, :, 0], kv[:, :, 1], k_sel, kv[:, :, 3]], axis=2)
    win = jnp.stack([k_win, kv[:, :, 5]], axis=2)
    return q, gates, rows, win, u


def even_finish(x, o_a, o_b, w_out, g_ffn, w1, w3, w2):
    x = x + jnp.concatenate([o_a, o_b], axis=-1) @ w_out
    return x + swiglu(rmsnorm(x, g_ffn), w1, w3, w2)


def odd_project(x, g_mix, w_in, b_in, ln_g, ln_b):
    z = jax.nn.gelu(rmsnorm(x, g_mix) @ w_in + b_in)
    return z[..., :GMLP_WIDTH], layernorm(z[..., GMLP_WIDTH:], ln_g, ln_b)


def spatial_mix(v, w_s, b_s):
    n = v.shape[2]
    w = jnp.where(jnp.tril(jnp.ones((n, n), bool))[None], w_s[:, :n, :n], 0.0).astype(v.dtype)
    return jnp.einsum('gij,bcjgd->bcigd', w, v) + b_s[:, :n].T[None, None, :, :, None]


def moe_ffn(h, w_router, w1, w3, w2):
    shape = h.shape
    hf = h.reshape(-1, shape[-1])
    logits = (hf @ w_router).astype(jnp.float32)
    top_v, top_i = lax.top_k(logits, TOP_K)
    wts = jax.nn.softmax(top_v, axis=-1)
    gate = jnp.einsum('nke,nk->ne', jax.nn.one_hot(top_i, N_EXPERTS, dtype=jnp.float32), wts).astype(h.dtype)
    y = jnp.zeros_like(hf)
    for e in range(N_EXPERTS):
        y = y + gate[:, e:e + 1] * swiglu(hf, w1[e], w3[e], w2[e])
    return y.reshape(shape)


def odd_finish(x, u, mix, w_out, g_ffn, w_router, w1, w3, w2):
    x = x + (u * mix) @ w_out
    return x + moe_ffn(rmsnorm(x, g_ffn), w_router, w1, w3, w2)


def setup_inputs(seed: int = 0) -> dict:
    key = jax.random.key(seed)
    keys = iter(jax.random.split(key, 40))

    def nrm(shape, scale):
        return jax.random.normal(next(keys), shape, jnp.float32) * scale

    def gain(shape):
        return 1.0 + nrm(shape, 0.02)

    n_pages = PAST_LEN // PAGE_SIZE
    n_used = DEC_BATCH * n_pages
    n_phys = (5 * n_used + 3) // 4
    perm = jax.random.permutation(next(keys), n_phys)
    page_table = perm[:n_used].reshape(DEC_BATCH, n_pages).astype(jnp.int32)
    win_buf = min(WINDOW, PAST_LEN)
    return {
        'x_prompt': nrm((BATCH, SEQ, D_MODEL), 1.0),
        'x_sample': nrm((DEC_BATCH, DEC_SEQ, D_MODEL), 1.0),
        'cache_kv': nrm((N_EVEN, n_phys, PAGE_SIZE, N_ROWS, N_KV_HEADS, HEAD_DIM), 1.0),
        'state_win': nrm((N_EVEN, DEC_BATCH, win_buf, 2, N_KV_HEADS, HEAD_DIM), 1.0),
        'state_pool': nrm((N_EVEN, DEC_BATCH, POOL_HIST, POOL_WIDTH), 1.0),
        'page_table': page_table,
        'g_mix_even': gain((N_EVEN, D_MODEL)),
        'w_in_even': nrm((N_EVEN, D_MODEL, IN_EVEN), D_MODEL ** -0.5),
        'g_q': gain((N_EVEN, HEAD_DIM)),
        'g_k': gain((N_EVEN, N_BRANCH, HEAD_DIM)),
        'w_cmp': nrm((N_EVEN, 2, CMP_BLOCK, HEAD_DIM, HEAD_DIM), (CMP_BLOCK * HEAD_DIM) ** -0.5),
        'pe_cmp': nrm((N_EVEN, 2, CMP_BLOCK, HEAD_DIM), 0.1),
        'w_pool': nrm((N_EVEN, len(POOL_WINDOWS), POOL_GROUP, POOL_GROUP), POOL_GROUP ** -0.5),
        'pool_scale': gain((N_EVEN, POOL_WIDTH)),
        'w_out_even': nrm((N_EVEN, D_MODEL, D_MODEL), D_MODEL ** -0.5),
        'g_ffn_even': gain((N_EVEN, D_MODEL)),
        'w_ffn1': nrm((N_EVEN, D_MODEL, D_FF), D_MODEL ** -0.5),
        'w_ffn3': nrm((N_EVEN, D_MODEL, D_FF), D_MODEL ** -0.5),
        'w_ffn2': nrm((N_EVEN, D_FF, D_MODEL), D_FF ** -0.5),
        'g_mix_odd': gain((N_ODD, D_MODEL)),
        'w_in_odd': nrm((N_ODD, D_MODEL, 2 * GMLP_WIDTH), D_MODEL ** -0.5),
        'b_in_odd': nrm((N_ODD, 2 * GMLP_WIDTH), 0.02),
        'ln_g': gain((N_ODD, GMLP_WIDTH)),
        'ln_b': nrm((N_ODD, GMLP_WIDTH), 0.02),
        'w_spatial': nrm((N_ODD, GMLP_GROUPS, CHUNK, CHUNK), CHUNK ** -0.5),
        'b_spatial': gain((N_ODD, GMLP_GROUPS, CHUNK)),
        'w_out_odd': nrm((N_ODD, GMLP_WIDTH, D_MODEL), GMLP_WIDTH ** -0.5),
        'g_ffn_odd': gain((N_ODD, D_MODEL)),
        'w_router': nrm((N_ODD, D_MODEL, N_EXPERTS), D_MODEL ** -0.5),
        'w_exp1': nrm((N_ODD, N_EXPERTS, D_MODEL, D_FF_EXPERT), D_MODEL ** -0.5),
        'w_exp3': nrm((N_ODD, N_EXPERTS, D_MODEL, D_FF_EXPERT), D_MODEL ** -0.5),
        'w_exp2': nrm((N_ODD, N_EXPERTS, D_FF_EXPERT, D_MODEL), D_FF_EXPERT ** -0.5),
    }


def reference(x_prompt, x_sample, cache_kv, state_win, state_pool, page_table,
              g_mix_even, w_in_even, g_q, g_k, w_cmp, pe_cmp, w_pool, pool_scale, w_out_even,
              g_ffn_even, w_ffn1, w_ffn3, w_ffn2,
              g_mix_odd, w_in_odd, b_in_odd, ln_g, ln_b, w_spatial, b_spatial, w_out_odd,
              g_ffn_odd, w_router, w_exp1, w_exp3, w_exp2):
    B, S, _ = x_prompt.shape
    DB, T, _ = x_sample.shape
    past_len = page_table.shape[1] * cache_kv.shape[2]
    win_buf = state_win.shape[2]
    xp, xs = x_prompt, x_sample
    kv_p, kv_s, win_p, win_s, pool_p, pool_s, gv_s = [], [], [], [], [], [], []
    for layer in range(DEPTH):
        i = layer // 2
        if layer % 2 == 0:
            q, gts, rows, win, u = even_project(xp, g_mix_even[i], w_in_even[i], g_q[i], g_k[i])
            o_a = nsa_prompt(q, gts, rows, win, w_cmp[i], pe_cmp[i], g_k[i, 0])
            o_b = pool_mix(u, jnp.zeros((B, POOL_HIST, POOL_WIDTH), u.dtype), 0, w_pool[i], pool_scale[i])
            xp = even_finish(xp, o_a, o_b, w_out_even[i], g_ffn_even[i], w_ffn1[i], w_ffn3[i], w_ffn2[i])
            kv_p.append(rows)
            win_p.append(win[:, -min(WINDOW, S):])
            pool_p.append(u[:, -POOL_HIST:])
            q, gts, rows, win, u = even_project(xs, g_mix_even[i], w_in_even[i], g_q[i], g_k[i])
            past = cache_kv[i][page_table].reshape(DB, past_len, N_ROWS, N_KV_HEADS, HEAD_DIM)
            rows_all = jnp.concatenate([past, rows], axis=1)
            win_all = jnp.concatenate([state_win[i], win], axis=1)
            o_a = nsa_sample(q, gts, rows_all, win_all, past_len, w_cmp[i], pe_cmp[i], g_k[i, 0])
            o_b = pool_mix(u, state_pool[i], past_len, w_pool[i], pool_scale[i])
            xs = even_finish(xs, o_a, o_b, w_out_even[i], g_ffn_even[i], w_ffn1[i], w_ffn3[i], w_ffn2[i])
            kv_s.append(rows)
            win_s.append(win_all[:, -win_buf:])
            pool_s.append(jnp.concatenate([state_pool[i], u], axis=1)[:, -POOL_HIST:])
        else:
            u, v = odd_project(xp, g_mix_odd[i], w_in_odd[i], b_in_odd[i], ln_g[i], ln_b[i])
            mix = spatial_mix(v.reshape(B, S // CHUNK, CHUNK, GMLP_GROUPS, GMLP_GROUP_DIM), w_spatial[i], b_spatial[i])
            xp = odd_finish(xp, u, mix.reshape(B, S, GMLP_WIDTH), w_out_odd[i], g_ffn_odd[i],
                            w_router[i], w_exp1[i], w_exp3[i], w_exp2[i])
            u, v = odd_project(xs, g_mix_odd[i], w_in_odd[i], b_in_odd[i], ln_g[i], ln_b[i])
            mix = spatial_mix(v.reshape(DB, 1, T, GMLP_GROUPS, GMLP_GROUP_DIM), w_spatial[i], b_spatial[i])
            xs = odd_finish(xs, u, mix.reshape(DB, T, GMLP_WIDTH), w_out_odd[i], g_ffn_odd[i],
                            w_router[i], w_exp1[i], w_exp3[i], w_exp2[i])
            gv_s.append(v)
    return (xp, xs, jnp.stack(kv_p), jnp.stack(kv_s), jnp.stack(win_p), jnp.stack(win_s),
            jnp.stack(pool_p), jnp.stack(pool_s), jnp.stack(gv_s))
```

```python
import functools

import jax
import jax.numpy as jnp
import numpy as np
from jax import lax
from jax.experimental import pallas as pl
from jax.experimental.pallas import tpu as pltpu

F32 = jnp.float32
BF16 = jnp.bfloat16

EPS = 1e-6
D_MODEL = 1024
N_HEADS = 8
HEAD_DIM = 64
N_KV_HEADS = 2
HEADS_PER_KV = N_HEADS // N_KV_HEADS
NSA_WIDTH = N_HEADS * HEAD_DIM
KV_WIDTH = N_KV_HEADS * HEAD_DIM
N_BRANCH = 3
CMP_BLOCK = 32
CMP_STRIDE = 16
SEL_BLOCK = 64
SEL_TOP = 16
SEL_LOCAL = 2
WINDOW = 512
Q_BLOCK = 128
POOL_WINDOWS = (2, 4, 8, 16)
POOL_WIDTH = D_MODEL - NSA_WIDTH
POOL_GROUP = POOL_WIDTH // len(POOL_WINDOWS)
POOL_HIST = max(POOL_WINDOWS) - 1
HIST_PAD = POOL_HIST + 1
CHUNK = 128
GMLP_WIDTH = D_MODEL
GMLP_GROUPS = 4
GMLP_GROUP_DIM = GMLP_WIDTH // GMLP_GROUPS
N_EXPERTS = 8
TOP_K = 2

LANES = 128
NEG = -1e30
VMEM_LIMIT = 56 * 1024 * 1024


def _cparams(*sem):
    return pltpu.CompilerParams(dimension_semantics=sem, vmem_limit_bytes=VMEM_LIMIT)


def _dot(a, b):
    return jnp.dot(a, b, preferred_element_type=F32)


def _dot_nt(a, b):
    return lax.dot_general(a, b, (((1,), (1,)), ((), ())), preferred_element_type=F32)


def _split_bf16(x):
    hi = x.astype(BF16)
    lo = (x - hi.astype(F32)).astype(BF16)
    return hi, lo


def _seg_mean_sq(y, bd):
    hi, lo = _split_bf16(y * y)
    return _dot(hi, bd) + _dot(lo, bd)


def _rms(x, g):
    return x * lax.rsqrt(jnp.mean(x * x, axis=-1, keepdims=True) + EPS) * g


def _block_diag_mean(width):
    seg = np.arange(width) // HEAD_DIM
    return jnp.asarray((seg[:, None] == seg[None, :]).astype(np.float32) / HEAD_DIM, dtype=BF16)


def _even_in_kernel(x_ref, g_ref, wq_ref, wr_ref, ww_ref, wg_ref, wu_ref, gq_ref, gks_ref, gkw_ref, bd_ref,
                    q_ref, rows_ref, win_ref, gates_ref, u_ref, kh_ref, vh_ref):
    xn = _rms(x_ref[...], g_ref[...]).astype(BF16)
    bd = bd_ref[...]
    bd_kv = bd[:KV_WIDTH, :KV_WIDTH]
    q = _dot(xn, wq_ref[...])
    q = q * lax.rsqrt(_seg_mean_sq(q, bd) + EPS) * gq_ref[...]
    q_ref[...] = (q * (HEAD_DIM ** -0.5)).astype(BF16)
    r = _dot(xn, wr_ref[...])
    k_sel = r[:, 2 * KV_WIDTH:3 * KV_WIDTH]
    k_sel = k_sel * lax.rsqrt(_seg_mean_sq(k_sel, bd_kv) + EPS) * gks_ref[...]
    v_sel = r[:, 3 * KV_WIDTH:]
    rows_ref[:, :2 * KV_WIDTH] = r[:, :2 * KV_WIDTH]
    rows_ref[:, 2 * KV_WIDTH:3 * KV_WIDTH] = k_sel
    rows_ref[:, 3 * KV_WIDTH:] = v_sel
    w = _dot(xn, ww_ref[...])
    k_win = w[:, :KV_WIDTH]
    k_win = k_win * lax.rsqrt(_seg_mean_sq(k_win, bd_kv) + EPS) * gkw_ref[...]
    v_win = w[:, KV_WIDTH:]
    win_ref[:, :KV_WIDTH] = k_win
    win_ref[:, KV_WIDTH:] = v_win
    gates_ref[...] = jax.nn.sigmoid(_dot(xn, wg_ref[...]))
    u_ref[...] = _dot(xn, wu_ref[...])
    ones = jnp.ones((x_ref.shape[0], HEAD_DIM), BF16)
    for j, (k, v) in enumerate(((k_sel, v_sel), (k_win, v_win))):
        for g in range(N_KV_HEADS):
            sl = slice(g * HEAD_DIM, (g + 1) * HEAD_DIM)
            kh_ref[j, g] = k[:, sl].astype(BF16)
            vh_ref[j, g] = jnp.concatenate([v[:, sl].astype(BF16), ones], axis=-1)


def _even_in(x, g_mix, w_in, g_q, g_k, tm):
    n = x.shape[0]
    o1 = NSA_WIDTH
    o2 = o1 + 6 * KV_WIDTH
    o3 = o2 + N_BRANCH * N_HEADS
    wb = w_in.astype(BF16)
    wq = wb[:, :o1]
    wr = wb[:, o1:o1 + 4 * KV_WIDTH]
    ww = wb[:, o1 + 4 * KV_WIDTH:o2]
    per_g = HEADS_PER_KV * N_BRANCH
    wg = jnp.zeros((D_MODEL, N_KV_HEADS * LANES), BF16)
    for g in range(N_KV_HEADS):
        wg = wg.at[:, g * LANES:g * LANES + per_g].set(wb[:, o2 + g * per_g:o2 + (g + 1) * per_g])
    wu = wb[:, o3:]
    gq = jnp.tile(g_q, N_HEADS)[None, :]
    gks = jnp.tile(g_k[1], N_KV_HEADS)[None, :]
    gkw = jnp.tile(g_k[2], N_KV_HEADS)[None, :]
    bd = _block_diag_mean(NSA_WIDTH)

    def full(a):
        return pl.BlockSpec(a.shape, lambda i: (0,) * a.ndim)

    def rows(width):
        return pl.BlockSpec((tm, width), lambda i: (i, 0))

    consts = (g_mix[None, :], wq, wr, ww, wg, wu, gq, gks, gkw, bd)
    return pl.pallas_call(
        _even_in_kernel,
        grid=(n // tm,),
        in_specs=[rows(D_MODEL)] + [full(a) for a in consts],
        out_specs=[rows(NSA_WIDTH), rows(4 * KV_WIDTH), rows(2 * KV_WIDTH), rows(N_KV_HEADS * LANES),
                   rows(POOL_WIDTH),
                   pl.BlockSpec((2, N_KV_HEADS, tm, HEAD_DIM), lambda i: (0, 0, i, 0)),
                   pl.BlockSpec((2, N_KV_HEADS, tm, 2 * HEAD_DIM), lambda i: (0, 0, i, 0))],
        out_shape=[jax.ShapeDtypeStruct((n, NSA_WIDTH), BF16),
                   jax.ShapeDtypeStruct((n, 4 * KV_WIDTH), F32),
                   jax.ShapeDtypeStruct((n, 2 * KV_WIDTH), F32),
                   jax.ShapeDtypeStruct((n, N_KV_HEADS * LANES), F32),
                   jax.ShapeDtypeStruct((n, POOL_WIDTH), F32),
                   jax.ShapeDtypeStruct((2, N_KV_HEADS, n, HEAD_DIM), BF16),
                   jax.ShapeDtypeStruct((2, N_KV_HEADS, n, 2 * HEAD_DIM), BF16)],
        compiler_params=_cparams("parallel"),
    )(x, *consts)


def _cmp_weights(w_cmp, pe_cmp):
    eye = jnp.eye(N_KV_HEADS, dtype=F32)
    w = [jnp.einsum('ab,lde->ladbe', eye, w_cmp[j]).reshape(CMP_BLOCK, KV_WIDTH, KV_WIDTH).astype(BF16)
         for j in range(2)]
    pe = [jnp.tile(pe_cmp[j], (1, N_KV_HEADS)) for j in range(2)]
    return w[0], w[1], pe[0], pe[1]


def _compress_rows(x_ref, w_ref, pe_ref, n_chunks):
    top = jnp.zeros((n_chunks, KV_WIDTH), F32)
    bot = jnp.zeros((n_chunks, KV_WIDTH), F32)
    for r in range(CMP_STRIDE):
        xr = x_ref[pl.ds(r, n_chunks, stride=CMP_STRIDE), :]
        top = top + _dot((xr + pe_ref[r:r + 1, :]).astype(BF16), w_ref[r])
        bot = bot + _dot((xr + pe_ref[CMP_STRIDE + r:CMP_STRIDE + r + 1, :]).astype(BF16), w_ref[CMP_STRIDE + r])
    return top + pltpu.roll(bot, n_chunks - 1, 0)


def _norm_kc(kc, gkc_ref, bd_ref):
    return kc * lax.rsqrt(_seg_mean_sq(kc, bd_ref[...]) + EPS) * gkc_ref[...]


def _compress_kernel(rk_ref, rv_ref, wk_ref, wv_ref, pek_ref, pev_ref, gkc_ref, bd_ref, kc_ref, vc_ref):
    n_chunks = rk_ref.shape[0] // CMP_STRIDE
    kc = _norm_kc(_compress_rows(rk_ref, wk_ref, pek_ref, n_chunks), gkc_ref, bd_ref)
    vc = _compress_rows(rv_ref, wv_ref, pev_ref, n_chunks)
    for g in range(N_KV_HEADS):
        sl = slice(g * HEAD_DIM, (g + 1) * HEAD_DIM)
        kc_ref[0, g] = kc[:, sl].astype(BF16)
        vc_ref[0, g] = vc[:, sl].astype(BF16)


def _compress_prompt(rows, n_seq, seq, cmp_w, g_kc):
    n_chunks = seq // CMP_STRIDE
    gkc = jnp.tile(g_kc, N_KV_HEADS)[None, :]
    bd = _block_diag_mean(KV_WIDTH)
    consts = tuple(cmp_w) + (gkc, bd)
    out_spec = pl.BlockSpec((1, N_KV_HEADS, n_chunks, HEAD_DIM), lambda b: (b, 0, 0, 0))
    out_shape = jax.ShapeDtypeStruct((n_seq, N_KV_HEADS, n_chunks, HEAD_DIM), BF16)
    return pl.pallas_call(
        _compress_kernel,
        grid=(n_seq,),
        in_specs=[pl.BlockSpec((seq, KV_WIDTH), lambda b: (b, 0)),
                  pl.BlockSpec((seq, KV_WIDTH), lambda b: (b, 1))]
        + [pl.BlockSpec(a.shape, lambda b, nd=a.ndim: (0,) * nd) for a in consts],
        out_specs=[out_spec, out_spec],
        out_shape=[out_shape, out_shape],
        compiler_params=_cparams("parallel"),
    )(rows, rows, *consts)


def _stack_heads(q):
    return jnp.concatenate([q[:, h * HEAD_DIM:(h + 1) * HEAD_DIM] for h in range(HEADS_PER_KV)], axis=0)


def _cmp_branch(qs, kc, vc, ov, t_pos, nq):
    n_c = kc.shape[0]
    c_end = lax.broadcasted_iota(jnp.int32, (1, n_c), 1) * CMP_STRIDE + (CMP_BLOCK - 1)
    mask = jnp.broadcast_to((c_end <= t_pos)[None], (HEADS_PER_KV, nq, n_c)).reshape(HEADS_PER_KV * nq, n_c)
    s = jnp.where(mask, _dot_nt(qs, kc), NEG)
    m = jnp.max(s, axis=-1, keepdims=True)
    e = jnp.where(mask, jnp.exp(s - m), 0.0)
    p = e / jnp.maximum(jnp.sum(e, axis=-1, keepdims=True), 1e-30)
    o_c = _dot(p.astype(BF16), vc)
    psum = jnp.sum(p.reshape(HEADS_PER_KV, nq, n_c), axis=0)
    hi, lo = _split_bf16(psum)
    return o_c, _dot(hi, ov) + _dot(lo, ov)


def _select_blocks(imp, t_pos, n_top):
    nq, nb = imp.shape
    blk = lax.broadcasted_iota(jnp.int32, (1, nb), 1)
    cur = t_pos // SEL_BLOCK
    valid = blk <= cur
    forced = valid & ((blk == 0) | (blk > cur - SEL_LOCAL))
    imp = jnp.where(forced, 1e30, jnp.where(valid, imp, -1e30))
    sel = jnp.zeros((nq, nb), jnp.bool_)
    for _ in range(n_top):
        m = jnp.max(imp, axis=-1, keepdims=True)
        idx = jnp.min(jnp.where(imp == m, blk, nb), axis=-1, keepdims=True)
        hit = blk == idx
        sel = sel | hit
        imp = jnp.where(hit, -3e38, imp)
    return jnp.where(sel, 1.0, 0.0).astype(BF16)


def _bias_rows(bias, nq):
    k = bias.shape[-1]
    return jnp.broadcast_to(bias[None], (HEADS_PER_KV, nq, k)).reshape(HEADS_PER_KV * nq, k)


def _combine(gt, o_c, o_s, o_w, nq):
    outs = []
    for h in range(HEADS_PER_KV):
        rs = slice(h * nq, (h + 1) * nq)
        c = h * N_BRANCH
        outs.append(gt[:, c:c + 1] * o_c[rs] + gt[:, c + 1:c + 2] * o_s[rs] + gt[:, c + 2:c + 3] * o_w[rs])
    return jnp.concatenate(outs, axis=-1)


def _overlap_matrix(n_cmp_pad, nb):
    c0 = np.arange(n_cmp_pad)[:, None] * CMP_STRIDE
    s0 = np.arange(nb)[None, :] * SEL_BLOCK
    return jnp.asarray(((c0 < s0 + SEL_BLOCK) & (c0 + CMP_BLOCK > s0)).astype(np.float32), dtype=BF16)


SEL_TILE = 256


def _nsa_prompt_kernel(q_ref, gt_ref, kc_ref, vc_ref, ks_ref, vs_ref, kw_ref, vw_ref, ov_ref, o_ref,
                       m_ref, acc_ref, *, seq, n_top):
    qi = pl.program_id(2)
    nq = Q_BLOCK
    q0 = qi * nq
    t_pos = q0 + lax.broadcasted_iota(jnp.int32, (nq, 1), 0)
    qs = _stack_heads(q_ref[...])
    nb = ov_ref.shape[1]

    o_c, imp = _cmp_branch(qs, kc_ref[0, 0], vc_ref[0, 0], ov_ref[...], t_pos, nq)
    selb = _select_blocks(imp, t_pos, n_top)

    m_ref[...] = jnp.full(m_ref.shape, NEG, F32)
    acc_ref[...] = jnp.zeros(acc_ref.shape, F32)
    per_tile = SEL_TILE // SEL_BLOCK
    jrow = lax.broadcasted_iota(jnp.int32, (nb, SEL_TILE), 0)
    kcol = lax.broadcasted_iota(jnp.int32, (nb, SEL_TILE), 1) // SEL_BLOCK
    key0 = lax.broadcasted_iota(jnp.int32, (1, SEL_TILE), 1)

    def sel_step(kt, carry):
        start = pl.multiple_of(kt * SEL_TILE, SEL_TILE)
        k = ks_ref[0, 0, pl.ds(start, SEL_TILE), :]
        v1 = vs_ref[0, 0, pl.ds(start, SEL_TILE), :]
        expand = jnp.where(jrow == kcol + kt * per_tile, 1.0, 0.0).astype(BF16)
        chosen = _dot(selb, expand) > 0.5
        ok = chosen & ((key0 + start) <= t_pos)
        s = _dot_nt(qs, k) + _bias_rows(jnp.where(ok, 0.0, NEG), nq)
        m_old = m_ref[...]
        m_new = jnp.maximum(m_old, jnp.max(s, axis=-1, keepdims=True))
        p = jnp.exp(s - m_new).astype(BF16)
        acc_ref[...] = acc_ref[...] * jnp.exp(m_old - m_new) + _dot(p, v1)
        m_ref[...] = m_new
        return carry

    lax.fori_loop(0, (q0 + nq - 1) // SEL_TILE + 1, sel_step, 0)
    acc = acc_ref[...]
    o_s = acc[:, :HEAD_DIM] / jnp.maximum(acc[:, HEAD_DIM:HEAD_DIM + 1], 1e-30)

    n_w = WINDOW + nq
    w0 = pl.multiple_of(jnp.maximum(q0 - WINDOW, 0), nq)
    kw = kw_ref[0, 0, pl.ds(w0, n_w), :]
    vw1 = vw_ref[0, 0, pl.ds(w0, n_w), :]
    dt = t_pos - (w0 + lax.broadcasted_iota(jnp.int32, (1, n_w), 1))
    s = _dot_nt(qs, kw) + _bias_rows(jnp.where((dt >= 0) & (dt < WINDOW), 0.0, NEG), nq)
    p = jnp.exp(s - jnp.max(s, axis=-1, keepdims=True)).astype(BF16)
    ow = _dot(p, vw1)
    o_w = ow[:, :HEAD_DIM] / jnp.maximum(ow[:, HEAD_DIM:HEAD_DIM + 1], 1e-30)

    o_ref[...] = _combine(gt_ref[...], o_c, o_s, o_w, nq).astype(o_ref.dtype)


def _nsa_prompt(q, gates, kc, vc, kh, vh, n_seq, seq):
    assert seq % SEL_TILE == 0 and seq >= WINDOW + Q_BLOCK
    n_q = seq // Q_BLOCK
    n_c = seq // CMP_STRIDE
    n_sel = seq // SEL_BLOCK
    nb = -(-n_sel // LANES) * LANES
    ov = _overlap_matrix(n_c, nb)
    gw = HEADS_PER_KV * HEAD_DIM
    cmp_spec = pl.BlockSpec((1, 1, n_c, HEAD_DIM), lambda b, g, i: (b, g, 0, 0))

    def kv_spec(j, width):
        return pl.BlockSpec((1, 1, seq, width), lambda b, g, i: (j, g, b, 0))

    kern = functools.partial(_nsa_prompt_kernel, seq=seq, n_top=min(SEL_TOP, n_sel))
    return pl.pallas_call(
        kern,
        grid=(n_seq, N_KV_HEADS, n_q),
        in_specs=[pl.BlockSpec((Q_BLOCK, gw), lambda b, g, i: (b * n_q + i, g)),
                  pl.BlockSpec((Q_BLOCK, LANES), lambda b, g, i: (b * n_q + i, g)),
                  cmp_spec, cmp_spec,
                  kv_spec(0, HEAD_DIM), kv_spec(0, 2 * HEAD_DIM), kv_spec(1, HEAD_DIM), kv_spec(1, 2 * HEAD_DIM),
                  pl.BlockSpec(ov.shape, lambda b, g, i: (0, 0))],
        out_specs=pl.BlockSpec((Q_BLOCK, gw), lambda b, g, i: (b * n_q + i, g)),
        out_shape=jax.ShapeDtypeStruct((n_seq * seq, NSA_WIDTH), BF16),
        scratch_shapes=[pltpu.VMEM((HEADS_PER_KV * Q_BLOCK, 1), F32),
                        pltpu.VMEM((HEADS_PER_KV * Q_BLOCK, 2 * HEAD_DIM), F32)],
        compiler_params=_cparams("parallel", "parallel", "arbitrary"),
    )(q, gates, kc, vc, kh, vh, kh, vh, ov)


def _nsa_sample_kernel(pt_ref, *refs, n_pages, page, n_new, past_len, n_top):
    del pt_ref
    pages = refs[:n_pages]
    (rows_ref, q_ref, gt_ref, sw_ref, wn_ref, wk_ref, wv_ref, pek_ref, pev_ref, gkc_ref, bd_ref, ov_ref, ex_ref,
     o_ref, *bufs) = refs[n_pages:]
    nq = n_new
    n_buf = bufs[2].shape[0]
    for j, buf in enumerate(bufs):
        for p in range(n_pages):
            buf[p * page:(p + 1) * page, :] = pages[p][0, :, j * KV_WIDTH:(j + 1) * KV_WIDTH]
    for j in (2, 3):
        bufs[j][past_len:past_len + nq, :] = rows_ref[0, :, j * KV_WIDTH:(j + 1) * KV_WIDTH]
        bufs[j][past_len + nq:, :] = jnp.zeros((n_buf - past_len - nq, KV_WIDTH), F32)

    n_chunks = past_len // CMP_STRIDE
    kc_all = _norm_kc(_compress_rows(bufs[0], wk_ref, pek_ref, n_chunks), gkc_ref, bd_ref).astype(BF16)
    vc_all = _compress_rows(bufs[1], wv_ref, pev_ref, n_chunks).astype(BF16)

    t_pos = past_len + lax.broadcasted_iota(jnp.int32, (nq, 1), 0)
    key = lax.broadcasted_iota(jnp.int32, (1, n_buf), 1)
    n_w = sw_ref.shape[1] + nq
    dt = t_pos - (past_len + nq - n_w + lax.broadcasted_iota(jnp.int32, (1, n_w), 1))
    w_bias = _bias_rows(jnp.where((dt >= 0) & (dt < WINDOW), 0.0, NEG), nq)
    k_win = jnp.concatenate([sw_ref[0, :, :KV_WIDTH], wn_ref[0, :, :KV_WIDTH]], axis=0).astype(BF16)
    v_win = jnp.concatenate([sw_ref[0, :, KV_WIDTH:], wn_ref[0, :, KV_WIDTH:]], axis=0).astype(BF16)
    k_sel = bufs[2][...].astype(BF16)
    v_sel = bufs[3][...].astype(BF16)
    q_all = q_ref[0]
    gt_all = gt_ref[0]

    outs = []
    for g in range(N_KV_HEADS):
        sl = slice(g * HEAD_DIM, (g + 1) * HEAD_DIM)
        qs = _stack_heads(q_all[:, g * HEADS_PER_KV * HEAD_DIM:(g + 1) * HEADS_PER_KV * HEAD_DIM])
        o_c, imp = _cmp_branch(qs, kc_all[:, sl], vc_all[:, sl], ov_ref[...], t_pos, nq)
        selb = _select_blocks(imp, t_pos, n_top)
        ok = (_dot(selb, ex_ref[...]) > 0.5) & (key <= t_pos)
        s = _dot_nt(qs, k_sel[:, sl]) + _bias_rows(jnp.where(ok, 0.0, NEG), nq)
        e = jnp.exp(s - jnp.max(s, axis=-1, keepdims=True))
        o_s = _dot(e.astype(BF16), v_sel[:, sl]) / jnp.maximum(jnp.sum(e, axis=-1, keepdims=True), 1e-30)
        s = _dot_nt(qs, k_win[:, sl]) + w_bias
        e = jnp.exp(s - jnp.max(s, axis=-1, keepdims=True))
        o_w = _dot(e.astype(BF16), v_win[:, sl]) / jnp.maximum(jnp.sum(e, axis=-1, keepdims=True), 1e-30)
        outs.append(_combine(gt_all[:, g * LANES:(g + 1) * LANES], o_c, o_s, o_w, nq))
    o_ref[0] = jnp.concatenate(outs, axis=-1).astype(o_ref.dtype)


def _nsa_sample(page_table, cache, rows_s, q_s, gates_s, state_win, win_off, win_s, cmp_w, g_kc):
    n_seq, n_pages = page_table.shape
    page = cache.shape[1]
    n_new = rows_s.shape[1]
    past_len = n_pages * page
    total = past_len + n_new
    n_cmp = (total - CMP_BLOCK) // CMP_STRIDE + 1
    n_chunks = past_len // CMP_STRIDE
    assert (n_cmp - 1) * CMP_STRIDE + CMP_BLOCK <= past_len and n_cmp == n_chunks - 1
    n_sel = -(-total // SEL_BLOCK)
    nb = -(-n_sel // LANES) * LANES
    n_buf = -(-n_sel * SEL_BLOCK // LANES) * LANES
    ov = _overlap_matrix(n_chunks, nb)
    ex = jnp.asarray((np.arange(nb)[:, None] == np.arange(n_buf)[None, :] // SEL_BLOCK).astype(np.float32), dtype=BF16)
    gkc = jnp.tile(g_kc, N_KV_HEADS)[None, :]
    bd = _block_diag_mean(KV_WIDTH)

    def page_spec(p):
        return pl.BlockSpec((1, page, cache.shape[2]), lambda b, pt: (pt[b, p], 0, 0))

    def seq_spec(a, off=0):
        return pl.BlockSpec((1,) + a.shape[1:], lambda b, pt: (b + off,) + (0,) * (a.ndim - 1))

    def full(a):
        return pl.BlockSpec(a.shape, lambda b, pt: (0,) * a.ndim)

    per_seq = (rows_s, q_s, gates_s, state_win, win_s)
    consts = tuple(cmp_w) + (gkc, bd, ov, ex)
    kern = functools.partial(_nsa_sample_kernel, n_pages=n_pages, page=page, n_new=n_new, past_len=past_len,
                             n_top=min(SEL_TOP, n_sel))
    return pl.pallas_call(
        kern,
        grid_spec=pltpu.PrefetchScalarGridSpec(
            num_scalar_prefetch=1,
            grid=(n_seq,),
            in_specs=([page_spec(p) for p in range(n_pages)]
                      + [seq_spec(a, win_off if a is state_win else 0) for a in per_seq] + [full(a) for a in consts]),
            out_specs=pl.BlockSpec((1, n_new, NSA_WIDTH), lambda b, pt: (b, 0, 0)),
            scratch_shapes=[pltpu.VMEM((past_len, KV_WIDTH), F32), pltpu.VMEM((past_len, KV_WIDTH), F32),
                            pltpu.VMEM((n_buf, KV_WIDTH), F32), pltpu.VMEM((n_buf, KV_WIDTH), F32)]),
        out_shape=jax.ShapeDtypeStruct((n_seq, n_new, NSA_WIDTH), BF16),
        compiler_params=_cparams("parallel"),
    )(page_table, *([cache] * n_pages), *per_seq, *consts)


def _pool_kernel(x_ref, h_ref, w_ref, sc_ref, o_ref, carry_ref, *, start_pos):
    j = pl.program_id(1)
    sb, tl, width = x_ref.shape

    @pl.when(j == 0)
    def _():
        carry_ref[...] = h_ref[...]

    ext3 = jnp.concatenate([carry_ref[...], x_ref[...]], axis=1)
    carry_ref[...] = ext3[:, tl:, :]
    n_ext = HIST_PAD + tl
    ext = ext3.reshape(sb * n_ext, width)
    pos = (start_pos + j * tl + lax.broadcasted_iota(jnp.int32, (tl, 1), 0)).astype(F32)
    outs = []
    for g, wlen in enumerate(POOL_WINDOWS):
        e = ext[:, g * POOL_GROUP:(g + 1) * POOL_GROUP]
        tot = e
        span = 1
        while span < wlen:
            tot = tot + pltpu.roll(tot, span, 0)
            span *= 2
        tot = tot.reshape(sb, n_ext, POOL_GROUP)[:, HIST_PAD:, :]
        cur = e.reshape(sb, n_ext, POOL_GROUP)[:, HIST_PAD:, :]
        cnt = jnp.minimum(float(wlen), pos + 1.0)
        d = (tot / cnt[None] - cur).reshape(sb * tl, POOL_GROUP).astype(BF16)
        outs.append(_dot(d, w_ref[g]))
    y = jnp.concatenate(outs, axis=-1) * sc_ref[...]
    o_ref[...] = y.reshape(sb, tl, width).astype(o_ref.dtype)


def _pool(u, hist, start_pos, w_pool, scale, sb, tl):
    n_seq, length, width = u.shape
    kern = functools.partial(_pool_kernel, start_pos=start_pos)
    return pl.pallas_call(
        kern,
        grid=(n_seq // sb, length // tl),
        in_specs=[pl.BlockSpec((sb, tl, width), lambda s, j: (s, j, 0)),
                  pl.BlockSpec((sb, HIST_PAD, width), lambda s, j: (s, 0, 0)),
                  pl.BlockSpec(w_pool.shape, lambda s, j: (0, 0, 0)),
                  pl.BlockSpec((1, width), lambda s, j: (0, 0))],
        out_specs=pl.BlockSpec((sb, tl, width), lambda s, j: (s, j, 0)),
        out_shape=jax.ShapeDtypeStruct(u.shape, BF16),
        scratch_shapes=[pltpu.VMEM((sb, HIST_PAD, width), F32)],
        compiler_params=_cparams("parallel", "arbitrary"),
    )(u, hist, w_pool.astype(BF16), scale[None, :])


def _proj_kernel(*refs, n_in, route):
    x_ref = refs[0]
    a_refs = refs[1:1 + n_in]
    w_refs = refs[1 + n_in:1 + 2 * n_in]
    g_ref = refs[1 + 2 * n_in]
    rest = refs[2 + 2 * n_in:]
    x = x_ref[...]
    for a_ref, w_ref in zip(a_refs, w_refs):
        x = x + _dot(a_ref[...], w_ref[...])
    h = _rms(x, g_ref[...])
    if not route:
        x1_ref, h_ref = rest
    else:
        wh_ref, wl_ref, x1_ref, h_ref, r_ref = rest
        hi, lo = _split_bf16(h)
        logits = _dot(hi, wh_ref[...]) + _dot(hi, wl_ref[...]) + _dot(lo, wh_ref[...])
        lane = lax.broadcasted_iota(jnp.int32, logits.shape, 1)
        lg = jnp.where(lane < N_EXPERTS, logits, NEG)
        m1 = jnp.max(lg, axis=-1, keepdims=True)
        i1 = jnp.min(jnp.where(lg == m1, lane, LANES), axis=-1, keepdims=True)
        lg = jnp.where(lane == i1, NEG, lg)
        m2 = jnp.max(lg, axis=-1, keepdims=True)
        i2 = jnp.min(jnp.where(lg == m2, lane, LANES), axis=-1, keepdims=True)
        e = jnp.exp(m2 - m1)
        w1 = 1.0 / (1.0 + e)
        w2 = e / (1.0 + e)
        r_ref[...] = (jnp.where(lane == i1, w1, 0.0) + jnp.where(lane == i2, w2, 0.0)
                      + jnp.where((lane == i1 + N_EXPERTS) | (lane == i2 + N_EXPERTS), 1.0, 0.0))
    x1_ref[...] = x
    h_ref[...] = h.astype(BF16)


def _proj(x, a_list, w_list, g, tm, w_router=None):
    n = x.shape[0]
    route = w_router is not None

    def rows(width):
        return pl.BlockSpec((tm, width), lambda i: (i, 0))

    def full(a):
        return pl.BlockSpec(a.shape, lambda i: (0,) * a.ndim)

    ins = [x] + list(a_list) + list(w_list) + [g[None, :]]
    in_specs = [rows(D_MODEL)] + [rows(a.shape[1]) for a in a_list] + [full(w) for w in w_list] + [full(g[None, :])]
    out_specs = [rows(D_MODEL), rows(D_MODEL)]
    out_shape = [jax.ShapeDtypeStruct((n, D_MODEL), F32), jax.ShapeDtypeStruct((n, D_MODEL), BF16)]
    if route:
        wr = jnp.zeros((D_MODEL, LANES), F32).at[:, :N_EXPERTS].set(w_router)
        wh, wl = _split_bf16(wr)
        ins += [wh, wl]
        in_specs += [full(wh), full(wl)]
        out_specs.append(rows(LANES))
        out_shape.append(jax.ShapeDtypeStruct((n, LANES), F32))
    kern = functools.partial(_proj_kernel, n_in=len(a_list), route=route)
    return pl.pallas_call(kern, grid=(n // tm,), in_specs=in_specs, out_specs=out_specs, out_shape=out_shape,
                          compiler_params=_cparams("parallel"))(*ins)


def _ffn_kernel(te_ref, h_ref, w1_ref, w3_ref, w2_ref, rs_ref, res_ref, o_ref, acc_ref):
    del te_ref
    k = pl.program_id(1)

    @pl.when(k == 0)
    def _():
        acc_ref[...] = jnp.zeros(acc_ref.shape, F32)

    h = h_ref[...]
    a = _dot(h, w1_ref[0])
    b = _dot(h, w3_ref[0])
    s = (a * jax.nn.sigmoid(a) * b).astype(BF16)
    acc_ref[...] += _dot(s, w2_ref[0])

    @pl.when(k == pl.num_programs(1) - 1)
    def _():
        o_ref[...] = res_ref[...] + rs_ref[...] * acc_ref[...]


def _ffn(h, w1, w3, w2, tile_expert, row_scale, resid, tm, tf):
    m = h.shape[0]
    d_ff = w1.shape[2]
    return pl.pallas_call(
        _ffn_kernel,
        grid_spec=pltpu.PrefetchScalarGridSpec(
            num_scalar_prefetch=1,
            grid=(m // tm, d_ff // tf),
            in_specs=[pl.BlockSpec((tm, D_MODEL), lambda i, k, te: (i, 0)),
                      pl.BlockSpec((1, D_MODEL, tf), lambda i, k, te: (te[i], 0, k)),
                      pl.BlockSpec((1, D_MODEL, tf), lambda i, k, te: (te[i], 0, k)),
                      pl.BlockSpec((1, tf, D_MODEL), lambda i, k, te: (te[i], k, 0)),
                      pl.BlockSpec((tm, 1), lambda i, k, te: (i, 0)),
                      (pl.BlockSpec((tm, D_MODEL), lambda i, k, te: (i, 0)) if resid.shape[0] == m else
                       pl.BlockSpec((1, 1), lambda i, k, te: (0, 0)))],
            out_specs=pl.BlockSpec((tm, D_MODEL), lambda i, k, te: (i, 0)),
            scratch_shapes=[pltpu.VMEM((tm, D_MODEL), F32)]),
        out_shape=jax.ShapeDtypeStruct((m, D_MODEL), F32),
        compiler_params=_cparams("parallel", "arbitrary"),
    )(tile_expert, h, w1, w3, w2, row_scale, resid)


def _odd_in_kernel(x_ref, g_ref, w_ref, b_ref, lg_ref, lb_ref, ms_ref, bs_ref, um_ref, v_ref, *, n_prompt_tiles):
    xn = _rms(x_ref[...], g_ref[...]).astype(BF16)
    z = jax.nn.gelu(_dot(xn, w_ref[...]) + b_ref[...], approximate=True)
    u = z[:, :GMLP_WIDTH]
    v = z[:, GMLP_WIDTH:]
    vc = v - jnp.mean(v, axis=-1, keepdims=True)
    v = vc * lax.rsqrt(jnp.mean(vc * vc, axis=-1, keepdims=True) + EPS) * lg_ref[...] + lb_ref[...]

    @pl.when(pl.program_id(0) >= n_prompt_tiles)
    def _():
        v_ref[...] = v

    vb = v.astype(BF16)
    bs = bs_ref[0]
    for c in range(x_ref.shape[0] // CHUNK):
        rs = slice(c * CHUNK, (c + 1) * CHUNK)
        for g in range(GMLP_GROUPS):
            cs = slice(g * GMLP_GROUP_DIM, (g + 1) * GMLP_GROUP_DIM)
            mix = _dot(ms_ref[0, g], vb[rs, cs]) + bs[:, g:g + 1]
            um_ref[rs, cs] = (u[rs, cs] * mix).astype(um_ref.dtype)


def _odd_in(x, g_mix, w_in, b_in, ln_g, ln_b, w_spatial, b_spatial, n_prompt, n_new, tm):
    n = x.shape[0]
    assert n_prompt % tm == 0 and tm % CHUNK == 0 and CHUNK % n_new == 0
    tri = jnp.tril(jnp.ones((CHUNK, CHUNK), bool))
    m_prompt = jnp.where(tri[None], w_spatial, 0.0)
    w_small = jnp.where(tri[None, :n_new, :n_new], w_spatial[:, :n_new, :n_new], 0.0)
    m_sample = jnp.einsum('ab,gij->gaibj', jnp.eye(CHUNK // n_new, dtype=F32), w_small).reshape(
        GMLP_GROUPS, CHUNK, CHUNK)
    ms = jnp.stack([m_prompt, m_sample]).astype(BF16)
    bs = jnp.stack([b_spatial.T, jnp.tile(b_spatial[:, :n_new].T, (CHUNK // n_new, 1))])
    n_prompt_tiles = n_prompt // tm

    def full(a):
        return pl.BlockSpec(a.shape, lambda i: (0,) * a.ndim)

    def variant(i):
        return jnp.where(i >= n_prompt_tiles, 1, 0)

    consts = (g_mix[None, :], w_in.astype(BF16), b_in[None, :], ln_g[None, :], ln_b[None, :])
    kern = functools.partial(_odd_in_kernel, n_prompt_tiles=n_prompt_tiles)
    return pl.pallas_call(
        kern,
        grid=(n // tm,),
        in_specs=[pl.BlockSpec((tm, D_MODEL), lambda i: (i, 0))] + [full(a) for a in consts] + [
            pl.BlockSpec((1,) + ms.shape[1:], lambda i: (variant(i), 0, 0, 0)),
            pl.BlockSpec((1,) + bs.shape[1:], lambda i: (variant(i), 0, 0))],
        out_specs=[pl.BlockSpec((tm, GMLP_WIDTH), lambda i: (i, 0)),
                   pl.BlockSpec((tm, GMLP_WIDTH), lambda i: (jnp.maximum(i - n_prompt_tiles, 0), 0))],
        out_shape=[jax.ShapeDtypeStruct((n, GMLP_WIDTH), BF16),
                   jax.ShapeDtypeStruct((n - n_prompt, GMLP_WIDTH), F32)],
        compiler_params=_cparams("arbitrary"),
    )(x, *consts, ms, bs)


def _route_plan(route, tm):
    n = route.shape[0]
    gate = route[:, :N_EXPERTS]
    sel = route[:, N_EXPERTS:2 * N_EXPERTS] > 0.5
    n_tiles = (TOP_K * n) // tm + N_EXPERTS
    m = n_tiles * tm
    seli = sel.astype(jnp.int32)
    rank = jnp.cumsum(seli, axis=0) - seli
    count = jnp.sum(seli, axis=0)
    padded = -(-count // tm) * tm
    ends = jnp.cumsum(padded)
    dest = jnp.where(sel, (ends - padded)[None, :] + rank, m)
    tok = jnp.broadcast_to(jnp.arange(n, dtype=jnp.int32)[:, None], dest.shape)
    src = jnp.zeros((m,), jnp.int32).at[dest.reshape(-1)].set(tok.reshape(-1), mode='drop')
    scale = jnp.zeros((m,), F32).at[dest.reshape(-1)].set(gate.reshape(-1), mode='drop')
    tile_expert = jnp.minimum(
        jnp.searchsorted(ends, jnp.arange(n_tiles, dtype=jnp.int32) * tm, side='right'), N_EXPERTS - 1
    ).astype(jnp.int32)
    slots = jnp.sort(dest, axis=1)[:, :TOP_K]
    return src, scale[:, None], tile_expert, slots


TM_PROJ = 512
TM_FFN = 512
TF_DENSE = 256
TF_EXPERT = 512
TM_EXPERT = 512


def kernel(x_prompt, x_sample, cache_kv, state_win, state_pool, page_table, g_mix_even, w_in_even, g_q, g_k, w_cmp, pe_cmp, w_pool, pool_scale, w_out_even, g_ffn_even, w_ffn1, w_ffn3, w_ffn2, g_mix_odd, w_in_odd, b_in_odd, ln_g, ln_b, w_spatial, b_spatial, w_out_odd, g_ffn_odd, w_router, w_exp1, w_exp3, w_exp2):
    n_b, seq, _ = x_prompt.shape
    n_db, n_new, _ = x_sample.shape
    n_prompt = n_b * seq
    n_sample = n_db * n_new
    n = n_prompt + n_sample
    n_phys, page = cache_kv.shape[1:3]
    past_len = page_table.shape[1] * page
    win_buf = state_win.shape[2]
    n_layers = w_in_even.shape[0] + w_in_odd.shape[0]
    x = jnp.concatenate([x_prompt.reshape(n_prompt, D_MODEL), x_sample.reshape(n_sample, D_MODEL)], axis=0)
    ones_col = jnp.ones((n, 1), F32)
    zero_tiles = jnp.zeros((n // TM_FFN,), jnp.int32)
    kv_p, kv_s, win_p, win_s, pool_p, pool_s, gv_s = [], [], [], [], [], [], []
    for layer in range(n_layers):
        i = layer // 2
        if layer % 2 == 0:
            q, rows, win, gates, u, kh, vh = _even_in(x, g_mix_even[i], w_in_even[i], g_q[i], g_k[i], TM_PROJ)
            cmp_w = _cmp_weights(w_cmp[i], pe_cmp[i])
            kc, vc = _compress_prompt(rows, n_b, seq, cmp_w, g_k[i, 0])
            oa_p = _nsa_prompt(q, gates, kc, vc, kh, vh, n_b, seq)
            rows_s = rows[n_prompt:].reshape(n_db, n_new, 4 * KV_WIDTH)
            win_new = win[n_prompt:].reshape(n_db, n_new, 2 * KV_WIDTH)
            sw = state_win[i].reshape(n_db, win_buf, 2 * KV_WIDTH)
            oa_s = _nsa_sample(page_table + i * n_phys, cache_kv.reshape(-1, page, 4 * KV_WIDTH), rows_s,
                               q[n_prompt:].reshape(n_db, n_new, NSA_WIDTH),
                               gates[n_prompt:].reshape(n_db, n_new, N_KV_HEADS * LANES),
                               state_win.reshape(-1, win_buf, 2 * KV_WIDTH), i * n_db, win_new, cmp_w,
                               g_k[i, 0])
            u_p = u[:n_prompt].reshape(n_b, seq, POOL_WIDTH)
            u_s = u[n_prompt:].reshape(n_db, n_new, POOL_WIDTH)
            hist_s = jnp.pad(state_pool[i], ((0, 0), (1, 0), (0, 0)))
            ob_p = _pool(u_p, jnp.zeros((n_b, HIST_PAD, POOL_WIDTH), F32), 0, w_pool[i], pool_scale[i], 1, 512)
            ob_s = _pool(u_s, hist_s, past_len, w_pool[i], pool_scale[i], n_db // 4, n_new)
            o_a = jnp.concatenate([oa_p, oa_s.reshape(n_sample, NSA_WIDTH)], axis=0)
            o_b = jnp.concatenate([ob_p.reshape(n_prompt, POOL_WIDTH), ob_s.reshape(n_sample, POOL_WIDTH)], axis=0)
            wo = w_out_even[i].astype(BF16)
            x1, h = _proj(x, [o_a, o_b], [wo[:NSA_WIDTH], wo[NSA_WIDTH:]], g_ffn_even[i], TM_PROJ)
            x = _ffn(h, w_ffn1[i].astype(BF16)[None], w_ffn3[i].astype(BF16)[None], w_ffn2[i].astype(BF16)[None],
                     zero_tiles, ones_col, x1, TM_FFN, TF_DENSE)
            kv_p.append(rows[:n_prompt].reshape(n_b, seq, 4, N_KV_HEADS, HEAD_DIM))
            kv_s.append(rows_s.reshape(n_db, n_new, 4, N_KV_HEADS, HEAD_DIM))
            n_keep = min(WINDOW, seq)
            win_p.append(win[:n_prompt].reshape(n_b, seq, 2, N_KV_HEADS, HEAD_DIM)[:, seq - n_keep:])
            win_s.append(jnp.concatenate([sw, win_new], axis=1)[:, n_new:].reshape(
                n_db, win_buf, 2, N_KV_HEADS, HEAD_DIM))
            pool_p.append(u_p[:, seq - POOL_HIST:])
            pool_s.append(jnp.concatenate([state_pool[i], u_s], axis=1)[:, n_new:])
        else:
            um, v_s = _odd_in(x, g_mix_odd[i], w_in_odd[i], b_in_odd[i], ln_g[i], ln_b[i], w_spatial[i],
                              b_spatial[i], n_prompt, n_new, TM_PROJ)
            x1, h, route = _proj(x, [um], [w_out_odd[i].astype(BF16)], g_ffn_odd[i], TM_PROJ, w_router[i])
            src, scale, tile_expert, slots = _route_plan(route, TM_EXPERT)
            hs = jnp.take(h, src, axis=0)
            ys = _ffn(hs, w_exp1[i].astype(BF16), w_exp3[i].astype(BF16), w_exp2[i].astype(BF16),
                      tile_expert, scale, jnp.zeros((1, 1), F32), TM_EXPERT, TF_EXPERT)
            x = x1 + jnp.sum(jnp.take(ys, slots, axis=0), axis=1)
            gv_s.append(v_s.reshape(n_db, n_new, GMLP_WIDTH))
    return (x[:n_prompt].reshape(n_b, seq, D_MODEL), x[n_prompt:].reshape(n_db, n_new, D_MODEL),
            jnp.stack(kv_p), jnp.stack(kv_s), jnp.stack(win_p), jnp.stack(win_s),
            jnp.stack(pool_p), jnp.stack(pool_s), jnp.stack(gv_s))
```
